```python
import jax, jax.numpy as jnp
from jax import lax
import numpy as np

D_MODEL = 1024
BATCH = 4
SEQ = 4096
DEPTH = 2
DEC_BATCH = 128
DEC_SEQ = 1
PAST_LEN = 2048
PAGE_SIZE = 128

HEAD_DIM = 64
N_SB_HEADS = 3 * D_MODEL // (4 * HEAD_DIM)
SB_WIDTH = N_SB_HEADS * HEAD_DIM
N_CONV_CH = SB_WIDTH
N_MEM_HEADS = 4
MEM_WIDTH = N_MEM_HEADS * HEAD_DIM
MIX_WIDTH = SB_WIDTH + MEM_WIDTH
N_MEM = 256
CONV_WIDTH = 31
D_FF = 2816
N_EXPERTS = 8
TOP_K = 2
D_FF_EXPERT = 1408
Q_BLOCK = 128
SB_BIAS_INIT = -6.0
N_A_LAYERS = DEPTH // 2
N_B_LAYERS = DEPTH - N_A_LAYERS
N_DENSE = (DEPTH + 1) // 2
N_MOE = DEPTH // 2
EPS = 1e-6

kernel_name = 'yoco_conformer_stickbreaking_decoder_step'


def rmsnorm(x, g):
    x32 = x.astype(jnp.float32)
    y = x32 * lax.rsqrt(jnp.mean(x32 * x32, axis=-1, keepdims=True) + EPS)
    return (y * g.astype(jnp.float32)).astype(x.dtype)


def layernorm(x, g, b):
    x32 = x.astype(jnp.float32)
    mu = jnp.mean(x32, axis=-1, keepdims=True)
    xc = x32 - mu
    y = xc * lax.rsqrt(jnp.mean(xc * xc, axis=-1, keepdims=True) + EPS)
    return (y * g.astype(jnp.float32) + b.astype(jnp.float32)).astype(x.dtype)


def swiglu(h, wg, wu, wd):
    return (jax.nn.silu(h @ wg) * (h @ wu)) @ wd


def moe_ffn(h, w_router, b_router, wg, wu, wd):
    logits = jnp.einsum('btd,de->bte', h, w_router, preferred_element_type=jnp.float32) + b_router.astype(jnp.float32)
    probs = jax.nn.softmax(logits, axis=-1)
    top_p, top_i = lax.top_k(probs, TOP_K)
    top_p = top_p / jnp.sum(top_p, axis=-1, keepdims=True)
    gates = jnp.sum(jax.nn.one_hot(top_i, N_EXPERTS, dtype=jnp.float32) * top_p[..., None], axis=-2)
    gates = gates.astype(h.dtype)
    y = jnp.zeros_like(h)
    for e in range(N_EXPERTS):
        y = y + gates[..., e:e + 1] * swiglu(h, wg[e], wu[e], wd[e])
    return y


def mem_attention(q, mem_k, mem_v):
    s = jnp.einsum('bthd,bmhd->bhtm', q, mem_k, preferred_element_type=jnp.float32) * HEAD_DIM ** -0.5
    p = jax.nn.softmax(s, axis=-1).astype(mem_v.dtype)
    return jnp.einsum('bhtm,bmhd->bthd', p, mem_v)


def _stick_breaking_block(q, k, v, bias, q_pos, k_pos):
    z = jnp.einsum('bthd,bshd->bhts', q, k, preferred_element_type=jnp.float32) * HEAD_DIM ** -0.5
    z = z + bias.astype(jnp.float32)[None, :, None, None]
    causal = k_pos[None, :] < q_pos[:, None]
    log_keep = jnp.where(causal, jax.nn.log_sigmoid(-z), 0.0)
    between = lax.cumsum(log_keep, axis=3, reverse=True) - log_keep
    w = jnp.where(causal, jnp.exp(jax.nn.log_sigmoid(z) + between), 0.0)
    return jnp.einsum('bhts,bshd->bthd', w.astype(v.dtype), v)


def stick_breaking_attention(q, k, v, bias):
    B, T, H, Dh = q.shape
    S = k.shape[1]
    k_pos = jnp.arange(S)
    q_pos = jnp.arange(T) + (S - T)
    if T % Q_BLOCK != 0:
        return _stick_breaking_block(q, k, v, bias, q_pos, k_pos)
    nb = T // Q_BLOCK
    qb = jnp.moveaxis(q.reshape(B, nb, Q_BLOCK, H, Dh), 1, 0)
    pb = q_pos.reshape(nb, Q_BLOCK)
    ob = lax.map(lambda a: _stick_breaking_block(a[0], k, v, bias, a[1], k_pos), (qb, pb))
    return jnp.moveaxis(ob, 0, 1).reshape(B, T, H, Dh)


def conv_module_tail(u_ext, conv_w, conv_b, ln_g, ln_b):
    c = lax.conv_general_dilated(u_ext, conv_w[:, None, :], (1,), 'VALID',
                                 dimension_numbers=('NWC', 'WIO', 'NWC'),
                                 feature_group_count=N_CONV_CH) + conv_b
    return jax.nn.silu(layernorm(c, ln_g, ln_b))


def mem_kv_projection(mem, g_mem, w_mem_kv):
    ks, vs = [], []
    B = mem.shape[0]
    for layer in range(DEPTH):
        kv = rmsnorm(mem, g_mem[layer]) @ w_mem_kv[layer]
        mk, mv = jnp.split(kv, [MEM_WIDTH], axis=-1)
        ks.append(mk.reshape(B, N_MEM, N_MEM_HEADS, HEAD_DIM))
        vs.append(mv.reshape(B, N_MEM, N_MEM_HEADS, HEAD_DIM))
    return jnp.stack(ks), jnp.stack(vs)


def trunk(x, mem_k, mem_v, conv_prev, kv_past, prm):
    B, T, _ = x.shape
    new_conv = []
    k_new = v_new = k_all = v_all = None
    for layer in range(DEPTH):
        h = rmsnorm(x, prm['g_pre_mix'][layer])
        if layer < N_A_LAYERS:
            a = layer
            proj = h @ prm['w_in_a'][a]
            ga, gb, qm = jnp.split(proj, [N_CONV_CH, 2 * N_CONV_CH], axis=-1)
            u = ga * jax.nn.sigmoid(gb)
            u_ext = jnp.concatenate([conv_prev[a], u], axis=1)
            new_conv.append(u_ext[:, -(CONV_WIDTH - 1):])
            o_main = conv_module_tail(u_ext, prm['conv_w'][a], prm['conv_b'][a],
                                      prm['conv_ln_g'][a], prm['conv_ln_b'][a])
            w_out = prm['w_out_a'][a]
        else:
            bi = layer - N_A_LAYERS
            proj = h @ prm['w_q_b'][bi]
            q, qm = jnp.split(proj, [SB_WIDTH], axis=-1)
            q = q.reshape(B, T, N_SB_HEADS, HEAD_DIM)
            o_main = stick_breaking_attention(q, k_all, v_all, prm['sb_bias'][bi]).reshape(B, T, SB_WIDTH)
            w_out = prm['w_out_b'][bi]
        o_mem = mem_attention(qm.reshape(B, T, N_MEM_HEADS, HEAD_DIM), mem_k[layer], mem_v[layer])
        mix = jnp.concatenate([o_main, o_mem.reshape(B, T, MEM_WIDTH)], axis=-1) @ w_out
        x = x + rmsnorm(mix, prm['g_post_mix'][layer])
        h = rmsnorm(x, prm['g_pre_ffn'][layer])
        i = layer // 2
        if layer % 2 == 0:
            f = swiglu(h, prm['w_gate_d'][i], prm['w_up_d'][i], prm['w_down_d'][i])
        else:
            f = moe_ffn(h, prm['w_router'][i], prm['b_router'][i],
                        prm['w_gate_e'][i], prm['w_up_e'][i], prm['w_down_e'][i])
        x = x + rmsnorm(f, prm['g_post_ffn'][layer])
        if layer == N_A_LAYERS - 1:
            kv = rmsnorm(x, prm['g_kv']) @ prm['w_kv']
            k_new, v_new = jnp.split(kv, [SB_WIDTH], axis=-1)
            k_new = k_new.reshape(B, T, N_SB_HEADS, HEAD_DIM)
            v_new = v_new.reshape(B, T, N_SB_HEADS, HEAD_DIM)
            if kv_past is None:
                k_all, v_all = k_new, v_new
            else:
                k_all = jnp.concatenate([kv_past[0], k_new], axis=1)
                v_all = jnp.concatenate([kv_past[1], v_new], axis=1)
    return x, new_conv, k_new, v_new


def setup_inputs(seed: int = 0) -> dict:
    key = jax.random.key(seed)
    ks = iter(jax.random.split(key, 48))
    f32 = jnp.float32

    def nrm(shape, scale):
        return jax.random.normal(next(ks), shape, f32) * scale

    def gain(shape):
        return 1.0 + nrm(shape, 0.02)

    n_pages = PAST_LEN // PAGE_SIZE
    n_used = DEC_BATCH * n_pages
    n_phys = n_used + n_used // 4
    page_table = jax.random.permutation(next(ks), n_phys)[:n_used].reshape(DEC_BATCH, n_pages).astype(jnp.int32)

    return {
        'x_prompt': nrm((BATCH, SEQ, D_MODEL), 1.0),
        'x_sample': nrm((DEC_BATCH, DEC_SEQ, D_MODEL), 1.0),
        'mem_prompt': nrm((BATCH, N_MEM, D_MODEL), 1.0),
        'cache_k': nrm((n_phys, PAGE_SIZE, N_SB_HEADS, HEAD_DIM), 1.0),
        'cache_v': nrm((n_phys, PAGE_SIZE, N_SB_HEADS, HEAD_DIM), 1.0),
        'page_table': page_table,
        'cache_mem_k': nrm((DEPTH, DEC_BATCH, N_MEM, N_MEM_HEADS, HEAD_DIM), 1.0),
        'cache_mem_v': nrm((DEPTH, DEC_BATCH, N_MEM, N_MEM_HEADS, HEAD_DIM), 1.0),
        'state_conv': nrm((N_A_LAYERS, DEC_BATCH, CONV_WIDTH - 1, N_CONV_CH), 0.5),
        'w_in_a': nrm((N_A_LAYERS, D_MODEL, 2 * N_CONV_CH + MEM_WIDTH), D_MODEL ** -0.5),
        'conv_w': nrm((N_A_LAYERS, CONV_WIDTH, N_CONV_CH), CONV_WIDTH ** -0.5),
        'conv_b': nrm((N_A_LAYERS, N_CONV_CH), 0.02),
        'conv_ln_g': gain((N_A_LAYERS, N_CONV_CH)),
        'conv_ln_b': nrm((N_A_LAYERS, N_CONV_CH), 0.02),
        'w_out_a': nrm((N_A_LAYERS, MIX_WIDTH, D_MODEL), MIX_WIDTH ** -0.5),
        'w_q_b': nrm((N_B_LAYERS, D_MODEL, SB_WIDTH + MEM_WIDTH), D_MODEL ** -0.5),
        'w_out_b': nrm((N_B_LAYERS, MIX_WIDTH, D_MODEL), MIX_WIDTH ** -0.5),
        'sb_bias': SB_BIAS_INIT + nrm((N_B_LAYERS, N_SB_HEADS), 0.1),
        'g_kv': gain((D_MODEL,)),
        'w_kv': nrm((D_MODEL, 2 * SB_WIDTH), D_MODEL ** -0.5),
        'g_mem': gain((DEPTH, D_MODEL)),
        'w_mem_kv': nrm((DEPTH, D_MODEL, 2 * MEM_WIDTH), D_MODEL ** -0.5),
        'g_pre_mix': gain((DEPTH, D_MODEL)),
        'g_post_mix': gain((DEPTH, D_MODEL)),
        'g_pre_ffn': gain((DEPTH, D_MODEL)),
        'g_post_ffn': gain((DEPTH, D_MODEL)),
        'w_gate_d': nrm((N_DENSE, D_MODEL, D_FF), D_MODEL ** -0.5),
        'w_up_d': nrm((N_DENSE, D_MODEL, D_FF), D_MODEL ** -0.5),
        'w_down_d': nrm((N_DENSE, D_FF, D_MODEL), D_FF ** -0.5),
        'w_router': nrm((N_MOE, D_MODEL, N_EXPERTS), D_MODEL ** -0.5),
        'b_router': nrm((N_MOE, N_EXPERTS), 0.01),
        'w_gate_e': nrm((N_MOE, N_EXPERTS, D_MODEL, D_FF_EXPERT), D_MODEL ** -0.5),
        'w_up_e': nrm((N_MOE, N_EXPERTS, D_MODEL, D_FF_EXPERT), D_MODEL ** -0.5),
        'w_down_e': nrm((N_MOE, N_EXPERTS, D_FF_EXPERT, D_MODEL), D_FF_EXPERT ** -0.5),
    }


def reference(x_prompt, x_sample, mem_prompt, cache_k, cache_v, page_table, cache_mem_k, cache_mem_v,
              state_conv, w_in_a, conv_w, conv_b, conv_ln_g, conv_ln_b, w_out_a, w_q_b, w_out_b,
              sb_bias, g_kv, w_kv, g_mem, w_mem_kv, g_pre_mix, g_post_mix, g_pre_ffn, g_post_ffn,
              w_gate_d, w_up_d, w_down_d, w_router, b_router, w_gate_e, w_up_e, w_down_e):
    prm = dict(w_in_a=w_in_a, conv_w=conv_w, conv_b=conv_b, conv_ln_g=conv_ln_g, conv_ln_b=conv_ln_b,
               w_out_a=w_out_a, w_q_b=w_q_b, w_out_b=w_out_b, sb_bias=sb_bias, g_kv=g_kv, w_kv=w_kv,
               g_pre_mix=g_pre_mix, g_post_mix=g_post_mix, g_pre_ffn=g_pre_ffn, g_post_ffn=g_post_ffn,
               w_gate_d=w_gate_d, w_up_d=w_up_d, w_down_d=w_down_d, w_router=w_router,
               b_router=b_router, w_gate_e=w_gate_e, w_up_e=w_up_e, w_down_e=w_down_e)

    mem_k_prompt, mem_v_prompt = mem_kv_projection(mem_prompt, g_mem, w_mem_kv)
    b_p = x_prompt.shape[0]
    conv0 = [jnp.zeros((b_p, CONV_WIDTH - 1, N_CONV_CH), x_prompt.dtype) for _ in range(N_A_LAYERS)]
    y_prompt, conv_p, k_prompt, v_prompt = trunk(x_prompt, mem_k_prompt, mem_v_prompt, conv0, None, prm)

    b_s, n_pages = page_table.shape
    past = n_pages * PAGE_SIZE
    k_past = cache_k[page_table].reshape(b_s, past, N_SB_HEADS, HEAD_DIM)
    v_past = cache_v[page_table].reshape(b_s, past, N_SB_HEADS, HEAD_DIM)
    conv_prev = [state_conv[a] for a in range(N_A_LAYERS)]
    y_sample, conv_s, k_sample, v_sample = trunk(x_sample, cache_mem_k, cache_mem_v, conv_prev,
                                                 (k_past, v_past), prm)

    conv_prompt = jnp.stack(conv_p)
    conv_sample = jnp.stack(conv_s)
    return (y_prompt, y_sample, conv_prompt, k_prompt, v_prompt, mem_k_prompt, mem_v_prompt,
            conv_sample, k_sample, v_sample)
```

```python
import functools

import jax
import jax.numpy as jnp
from jax import lax
from jax.experimental import pallas as pl
from jax.experimental.pallas import tpu as pltpu

F32 = jnp.float32
BF16 = jnp.bfloat16

D_MODEL = 1024
HEAD_DIM = 64
N_SB_HEADS = 12
SB_WIDTH = N_SB_HEADS * HEAD_DIM
N_MEM_HEADS = 4
MEM_WIDTH = N_MEM_HEADS * HEAD_DIM
N_MEM = 256
CONV_WIDTH = 31
N_EXPERTS = 8
PAGE_SIZE = 128
EPS = 1e-6
QK_SCALE = HEAD_DIM ** -0.5

LANES = 128
HALO = 32
CONV_CHUNK = 32
SB_BLOCK = 256
VMEM_LIMIT = 56 * 1024 * 1024

NT_DIMS = (((1,), (1,)), ((), ()))


def _dot(a, b):
    return jnp.dot(a, b, preferred_element_type=F32)


def _dot_nt(a, b):
    return lax.dot_general(a, b, NT_DIMS, preferred_element_type=F32)


def _inv_rms(x):
    return lax.rsqrt(jnp.mean(x * x, axis=-1, keepdims=True) + EPS)


def _params(n_parallel, n_arbitrary=0):
    sem = ("parallel",) * n_parallel + ("arbitrary",) * n_arbitrary
    return pltpu.CompilerParams(dimension_semantics=sem, vmem_limit_bytes=VMEM_LIMIT)


def _row_spec(tm, width):
    return pl.BlockSpec((tm, width), lambda i: (i, 0))


def _full_spec(shape):
    return pl.BlockSpec(shape, lambda i: (0,) * len(shape))


def _norm_matmul_kernel(x_ref, g_ref, w_ref, o_ref):
    x = x_ref[...]
    h = (x * _inv_rms(x) * g_ref[...]).astype(BF16)
    o_ref[...] = _dot(h, w_ref[...])


def norm_matmul(x, g, w, tm):
    n, d = x.shape
    m = w.shape[1]
    return pl.pallas_call(
        _norm_matmul_kernel,
        grid=(n // tm,),
        in_specs=[_row_spec(tm, d), _full_spec((1, d)), _full_spec((d, m))],
        out_specs=_row_spec(tm, m),
        out_shape=jax.ShapeDtypeStruct((n, m), F32),
        compiler_params=_params(1),
        name="norm_matmul",
    )(x, g, w)


def _inproj_kernel(x_ref, g_ref, wa_ref, wb_ref, wq_ref, u_ref, qm_ref):
    x = x_ref[...]
    h = (x * _inv_rms(x) * g_ref[...]).astype(BF16)
    a = _dot(h, wa_ref[...])
    b = _dot(h, wb_ref[...])
    u_ref[...] = a * jax.nn.sigmoid(b)
    qm_ref[...] = _dot(h, wq_ref[...])


def inproj_glu(x, g, wa, wb, wq, tm):
    n, d = x.shape
    c = wa.shape[1]
    return pl.pallas_call(
        _inproj_kernel,
        grid=(n // tm,),
        in_specs=[_row_spec(tm, d), _full_spec((1, d)), _full_spec((d, c)), _full_spec((d, c)),
                  _full_spec((d, MEM_WIDTH))],
        out_specs=[_row_spec(tm, c), _row_spec(tm, MEM_WIDTH)],
        out_shape=[jax.ShapeDtypeStruct((n, c), F32), jax.ShapeDtypeStruct((n, MEM_WIDTH), F32)],
        compiler_params=_params(1),
        name="inproj_glu",
    )(x, g, wa, wb, wq)


def _kvq_kernel(x_ref, gkv_ref, gq_ref, wk_ref, wv_ref, wq_ref, wqm_ref,
                k_ref, v_ref, kb_ref, vb_ref, qb_ref, qm_ref):
    x = x_ref[...]
    xn = x * _inv_rms(x)
    hkv = (xn * gkv_ref[...]).astype(BF16)
    hq = (xn * gq_ref[...]).astype(BF16)
    k = _dot(hkv, wk_ref[...])
    v = _dot(hkv, wv_ref[...])
    k_ref[...] = k
    v_ref[...] = v
    kb_ref[...] = k.astype(BF16)
    vb_ref[...] = v.astype(BF16)
    qb_ref[...] = (_dot(hq, wq_ref[...]) * QK_SCALE).astype(BF16)
    qm_ref[...] = _dot(hq, wqm_ref[...])


def kvq_proj(x, gkv, gq, wk, wv, wq, wqm, tm):
    n, d = x.shape
    w = SB_WIDTH
    return pl.pallas_call(
        _kvq_kernel,
        grid=(n // tm,),
        in_specs=[_row_spec(tm, d), _full_spec((1, d)), _full_spec((1, d)), _full_spec((d, w)),
                  _full_spec((d, w)), _full_spec((d, w)), _full_spec((d, MEM_WIDTH))],
        out_specs=[_row_spec(tm, w)] * 5 + [_row_spec(tm, MEM_WIDTH)],
        out_shape=[jax.ShapeDtypeStruct((n, w), F32), jax.ShapeDtypeStruct((n, w), F32),
                   jax.ShapeDtypeStruct((n, w), BF16), jax.ShapeDtypeStruct((n, w), BF16),
                   jax.ShapeDtypeStruct((n, w), BF16), jax.ShapeDtypeStruct((n, MEM_WIDTH), F32)],
        compiler_params=_params(1),
        name="kvq_proj",
    )(x, gkv, gq, wk, wv, wq, wqm)


def _ln_swish(c, lg, lb):
    mu = jnp.mean(c, axis=-1, keepdims=True)
    xc = c - mu
    y = xc * lax.rsqrt(jnp.mean(xc * xc, axis=-1, keepdims=True) + EPS) * lg + lb
    return y * jax.nn.sigmoid(y)


def _conv_prompt_kernel(u_ref, halo_ref, w_ref, cb_ref, lg_ref, lb_ref, o_ref, ext_ref, *, tt):
    i = pl.program_id(1)
    ext_ref[0:HALO, :] = jnp.where(i == 0, 0.0, halo_ref[0])
    ext_ref[HALO:HALO + tt, :] = u_ref[0]
    first = HALO - (CONV_WIDTH - 1)
    for r in range(tt // CONV_CHUNK):
        base = r * CONV_CHUNK + first
        acc = ext_ref[base:base + CONV_CHUNK, :] * w_ref[0:1, :]
        for k in range(1, CONV_WIDTH):
            acc = acc + ext_ref[base + k:base + k + CONV_CHUNK, :] * w_ref[k:k + 1, :]
        c = acc + cb_ref[...]
        o_ref[0, r * CONV_CHUNK:(r + 1) * CONV_CHUNK, :] = _ln_swish(c, lg_ref[...], lb_ref[...])


def conv_prompt(u, w, cb, lg, lb, tt):
    b, t, c = u.shape
    per = tt // HALO
    return pl.pallas_call(
        functools.partial(_conv_prompt_kernel, tt=tt),
        grid=(b, t // tt),
        in_specs=[pl.BlockSpec((1, tt, c), lambda bi, i: (bi, i, 0)),
                  pl.BlockSpec((1, HALO, c), lambda bi, i: (bi, jnp.maximum(i * per - 1, 0), 0)),
                  pl.BlockSpec((HALO, c), lambda bi, i: (0, 0)),
                  pl.BlockSpec((1, c), lambda bi, i: (0, 0)),
                  pl.BlockSpec((1, c), lambda bi, i: (0, 0)),
                  pl.BlockSpec((1, c), lambda bi, i: (0, 0))],
        out_specs=pl.BlockSpec((1, tt, c), lambda bi, i: (bi, i, 0)),
        out_shape=jax.ShapeDtypeStruct((b, t, c), F32),
        scratch_shapes=[pltpu.VMEM((HALO + tt, c), F32)],
        compiler_params=_params(2),
        name="conv_prompt",
    )(u, u, w, cb, lg, lb)


def _conv_sample_kernel(s_ref, u_ref, w_ref, cb_ref, lg_ref, lb_ref, o_ref, ns_ref, *, bs):
    n_prev = CONV_WIDTH - 1
    for b in range(bs):
        s = s_ref[b]
        u = u_ref[b:b + 1, :]
        c = jnp.sum(s * w_ref[0:n_prev, :], axis=0, keepdims=True) + u * w_ref[n_prev:n_prev + 1, :]
        c = c + cb_ref[...]
        o_ref[b:b + 1, :] = _ln_swish(c, lg_ref[...], lb_ref[...])
        ns_ref[b, 0:n_prev - 1, :] = s_ref[b, 1:n_prev, :]
        ns_ref[b, n_prev - 1:n_prev, :] = u


def conv_sample(state, u, w, cb, lg, lb, bs):
    n, n_prev, c = state.shape
    return pl.pallas_call(
        functools.partial(_conv_sample_kernel, bs=bs),
        grid=(n // bs,),
        in_specs=[pl.BlockSpec((bs, n_prev, c), lambda i: (i, 0, 0)),
                  _row_spec(bs, c), _full_spec((HALO, c)), _full_spec((1, c)), _full_spec((1, c)),
                  _full_spec((1, c))],
        out_specs=[_row_spec(bs, c), pl.BlockSpec((bs, n_prev, c), lambda i: (i, 0, 0))],
        out_shape=[jax.ShapeDtypeStruct((n, c), F32), jax.ShapeDtypeStruct((n, n_prev, c), F32)],
        compiler_params=_params(1),
        name="conv_sample",
    )(state, u, w, cb, lg, lb)


def _softmax_rows(s):
    e = jnp.exp(s - jnp.max(s, axis=-1, keepdims=True))
    return e / jnp.sum(e, axis=-1, keepdims=True)


def _mem_attn_prompt_kernel(q_ref, mk_ref, mv_ref, o_ref):
    q = q_ref[0] * QK_SCALE
    mk = mk_ref[0]
    mv = mv_ref[0]
    head = lax.broadcasted_iota(jnp.int32, q.shape, 1) // HEAD_DIM
    acc = jnp.zeros(q.shape, F32)
    for h in range(N_MEM_HEADS):
        sel = head == h
        s = _dot_nt(jnp.where(sel, q, 0.0).astype(BF16), mk)
        o = _dot(_softmax_rows(s).astype(BF16), mv)
        acc = jnp.where(sel, o, acc)
    o_ref[0] = acc


def mem_attn_prompt(qm, mk, mv, tt):
    b, t, w = qm.shape
    return pl.pallas_call(
        _mem_attn_prompt_kernel,
        grid=(b, t // tt),
        in_specs=[pl.BlockSpec((1, tt, w), lambda bi, i: (bi, i, 0)),
                  pl.BlockSpec((1, N_MEM, w), lambda bi, i: (bi, 0, 0)),
                  pl.BlockSpec((1, N_MEM, w), lambda bi, i: (bi, 0, 0))],
        out_specs=pl.BlockSpec((1, tt, w), lambda bi, i: (bi, i, 0)),
        out_shape=jax.ShapeDtypeStruct((b, t, w), F32),
        compiler_params=_params(2),
        name="mem_attn_prompt",
    )(qm, mk, mv)


def _mem_attn_sample_kernel(q_ref, mk_ref, mv_ref, o_ref, *, bs):
    rows = 8
    row = lax.broadcasted_iota(jnp.int32, (rows, MEM_WIDTH), 0)
    head = lax.broadcasted_iota(jnp.int32, (rows, MEM_WIDTH), 1) // HEAD_DIM
    sel = row == head
    for b in range(bs):
        q = jnp.broadcast_to(q_ref[b:b + 1, :] * QK_SCALE, (rows, MEM_WIDTH))
        qh = jnp.where(sel, q, 0.0).astype(BF16)
        s = _dot_nt(qh, mk_ref[b])
        o = _dot(_softmax_rows(s).astype(BF16), mv_ref[b])
        o_ref[b:b + 1, :] = jnp.sum(jnp.where(sel, o, 0.0), axis=0, keepdims=True)


def mem_attn_sample(qm, mk, mv, bs):
    n, w = qm.shape
    return pl.pallas_call(
        functools.partial(_mem_attn_sample_kernel, bs=bs),
        grid=(n // bs,),
        in_specs=[_row_spec(bs, w),
                  pl.BlockSpec((bs, N_MEM, w), lambda i: (i, 0, 0)),
                  pl.BlockSpec((bs, N_MEM, w), lambda i: (i, 0, 0))],
        out_specs=_row_spec(bs, w),
        out_shape=jax.ShapeDtypeStruct((n, w), F32),
        compiler_params=_params(1),
        name="mem_attn_sample",
    )(qm, mk, mv)


def _outproj_kernel(om_ref, omem_ref, w1_ref, w2_ref, x_ref, g_ref, o_ref):
    mix = _dot(om_ref[...].astype(BF16), w1_ref[...]) + _dot(omem_ref[...].astype(BF16), w2_ref[...])
    o_ref[...] = x_ref[...] + mix * _inv_rms(mix) * g_ref[...]


def outproj(o_main, o_mem, w1, w2, x, g, tm):
    n, d = x.shape
    return pl.pallas_call(
        _outproj_kernel,
        grid=(n // tm,),
        in_specs=[_row_spec(tm, SB_WIDTH), _row_spec(tm, MEM_WIDTH), _full_spec((SB_WIDTH, d)),
                  _full_spec((MEM_WIDTH, d)), _row_spec(tm, d), _full_spec((1, d))],
        out_specs=_row_spec(tm, d),
        out_shape=jax.ShapeDtypeStruct((n, d), F32),
        compiler_params=_params(1),
        name="outproj",
    )(o_main, o_mem, w1, w2, x, g)


def _top2_gates(logits):
    lane = lax.broadcasted_iota(jnp.int32, logits.shape, 1).astype(F32)
    valid = lane < N_EXPERTS
    lg = jnp.where(valid, logits, -1e30)
    e = jnp.where(valid, jnp.exp(lg - jnp.max(lg, axis=-1, keepdims=True)), 0.0)
    p = e / jnp.sum(e, axis=-1, keepdims=True)
    p = jnp.where(valid, p, -1.0)
    m1 = jnp.max(p, axis=-1, keepdims=True)
    i1 = jnp.min(jnp.where(p == m1, lane, float(LANES)), axis=-1, keepdims=True)
    p2 = jnp.where(lane == i1, -1.0, p)
    m2 = jnp.max(p2, axis=-1, keepdims=True)
    i2 = jnp.min(jnp.where(p2 == m2, lane, float(LANES)), axis=-1, keepdims=True)
    tot = m1 + m2
    return jnp.where(lane == i1, m1 / tot, jnp.where(lane == i2, m2 / tot, 0.0))


def _ffn_kernel(x_ref, gpre_ref, gpost_ref, wrh_ref, wrl_ref, br_ref, wg_ref, wu_ref, wd_ref,
                o_ref, h_ref, acc_ref, gate_ref, *, moe, n_steps):
    j = pl.program_id(1)

    @pl.when(j == 0)
    def _():
        x = x_ref[...]
        h = x * _inv_rms(x) * gpre_ref[...]
        h_hi = h.astype(BF16)
        h_ref[...] = h_hi
        acc_ref[...] = jnp.zeros_like(acc_ref)
        if moe:
            h_lo = (h - h_hi.astype(F32)).astype(BF16)
            logits = (_dot(h_hi, wrh_ref[...]) + _dot(h_hi, wrl_ref[...]) + _dot(h_lo, wrh_ref[...])
                      + br_ref[...])
            gate_ref[...] = _top2_gates(logits)

    h = h_ref[...]
    g = _dot(h, wg_ref[0])
    u = _dot(h, wu_ref[0])
    a = ((g * jax.nn.sigmoid(g)) * u).astype(BF16)
    y = _dot(a, wd_ref[0])
    if moe:
        lane = lax.broadcasted_iota(jnp.int32, gate_ref.shape, 1)
        y = y * jnp.sum(jnp.where(lane == j, gate_ref[...], 0.0), axis=-1, keepdims=True)
    acc_ref[...] += y

    @pl.when(j == n_steps - 1)
    def _():
        f = acc_ref[...]
        o_ref[...] = x_ref[...] + f * _inv_rms(f) * gpost_ref[...]


def ffn(x, gpre, gpost, wrh, wrl, br, wg, wu, wd, tm, moe):
    n, d = x.shape
    if moe:
        n_steps, f = wg.shape[0], wg.shape[2]
        up_map = lambda i, j: (j, 0, 0)
        down_map = lambda i, j: (j, 0, 0)
    else:
        f = wd.shape[1] // 2
        n_steps = 2
        up_map = lambda i, j: (0, 0, j)
        down_map = lambda i, j: (0, j, 0)
    const = lambda i, j: (0, 0)
    return pl.pallas_call(
        functools.partial(_ffn_kernel, moe=moe, n_steps=n_steps),
        grid=(n // tm, n_steps),
        in_specs=[pl.BlockSpec((tm, d), lambda i, j: (i, 0)),
                  pl.BlockSpec((1, d), const), pl.BlockSpec((1, d), const),
                  pl.BlockSpec((d, LANES), const), pl.BlockSpec((d, LANES), const),
                  pl.BlockSpec((1, LANES), const),
                  pl.BlockSpec((1, d, f), up_map), pl.BlockSpec((1, d, f), up_map),
                  pl.BlockSpec((1, f, d), down_map)],
        out_specs=pl.BlockSpec((tm, d), lambda i, j: (i, 0)),
        out_shape=jax.ShapeDtypeStruct((n, d), F32),
        scratch_shapes=[pltpu.VMEM((tm, d), BF16), pltpu.VMEM((tm, d), F32),
                        pltpu.VMEM((tm, LANES), F32)],
        compiler_params=_params(1, 1),
        name="ffn_moe" if moe else "ffn_dense",
    )(x, gpre, gpost, wrh, wrl, br, wg, wu, wd)


def _sb_weights(z, carry, tri, causal):
    soft = jnp.log(1.0 + jnp.exp(-jnp.abs(z)))
    log_beta = jnp.minimum(z, 0.0) - soft
    log_keep = log_beta - z
    if causal is not None:
        log_keep = jnp.where(causal, log_keep, 0.0)
    hi = log_keep.astype(BF16)
    lo = (log_keep - hi.astype(F32)).astype(BF16)
    between = _dot(hi, tri) + _dot(lo, tri) + carry
    w = jnp.exp(log_beta + between)
    if causal is not None:
        w = jnp.where(causal, w, 0.0)
    return w, carry + jnp.sum(log_keep, axis=-1, keepdims=True)


def _sb_prompt_kernel(bias_ref, q_ref, k_ref, v_ref, tri_ref, o_ref):
    hp = pl.program_id(1)
    qi = pl.program_id(2)
    blk = SB_BLOCK
    q = q_ref[0].astype(F32)
    tri = tri_ref[...]
    head = lax.broadcasted_iota(jnp.int32, (blk, LANES), 1) // HEAD_DIM
    row = lax.broadcasted_iota(jnp.int32, (blk, blk), 0)
    col = lax.broadcasted_iota(jnp.int32, (blk, blk), 1)
    causal = col < row
    outs = []
    for hh in range(LANES // HEAD_DIM):
        qh = jnp.where(head == hh, q, 0.0).astype(BF16)
        bias = bias_ref[hp * (LANES // HEAD_DIM) + hh]

        def block(kb, carry, acc, mask, qh=qh, bias=bias):
            start = pl.multiple_of(kb * blk, blk)
            z = _dot_nt(qh, k_ref[0, pl.ds(start, blk), :]) + bias
            w, carry = _sb_weights(z, carry, tri, mask)
            return carry, acc + _dot(w.astype(BF16), v_ref[0, pl.ds(start, blk), :])

        carry, acc = block(qi, jnp.zeros((blk, 1), F32), jnp.zeros((blk, LANES), F32), causal)

        def body(i, c, block=block):
            return block(qi - 1 - i, c[0], c[1], None)

        carry, acc = lax.fori_loop(0, qi, body, (carry, acc))
        outs.append(acc)
    o_ref[0] = jnp.where(head == 0, outs[0], outs[1])


def sb_attn_prompt(bias, qb, kb, vb, tri):
    b, t, w = qb.shape
    blk = SB_BLOCK
    grid_spec = pltpu.PrefetchScalarGridSpec(
        num_scalar_prefetch=0,
        grid=(b, w // LANES, t // blk),
        in_specs=[pl.BlockSpec(memory_space=pltpu.SMEM),
                  pl.BlockSpec((1, blk, LANES), lambda bi, hp, qi: (bi, qi, hp)),
                  pl.BlockSpec((1, t, LANES), lambda bi, hp, qi: (bi, 0, hp)),
                  pl.BlockSpec((1, t, LANES), lambda bi, hp, qi: (bi, 0, hp)),
                  pl.BlockSpec((blk, blk), lambda bi, hp, qi: (0, 0))],
        out_specs=pl.BlockSpec((1, blk, LANES), lambda bi, hp, qi: (bi, qi, hp)),
    )
    return pl.pallas_call(
        _sb_prompt_kernel,
        grid_spec=grid_spec,
        out_shape=jax.ShapeDtypeStruct((b, t, w), F32),
        compiler_params=_params(3),
        name="sb_attn_prompt",
    )(bias, qb, kb, vb, tri)


def _sb_sample_kernel(pt_ref, q_ref, bias_ref, tri_ref, *refs, n_pages):
    k_refs = refs[:n_pages]
    v_refs = refs[n_pages:2 * n_pages]
    o_ref = refs[2 * n_pages]
    rows = 16
    row = lax.broadcasted_iota(jnp.int32, (rows, SB_WIDTH), 0)
    head = lax.broadcasted_iota(jnp.int32, (rows, SB_WIDTH), 1) // HEAD_DIM
    sel = row == head
    q = jnp.broadcast_to(q_ref[0].astype(F32), (rows, SB_WIDTH))
    qh = jnp.where(sel, q, 0.0).astype(BF16)
    tri = tri_ref[...]
    bias = bias_ref[...]
    carry = jnp.zeros((rows, 1), F32)
    acc = jnp.zeros((rows, SB_WIDTH), F32)
    for p in reversed(range(n_pages)):
        z = _dot_nt(qh, k_refs[p][0]) + bias
        w, carry = _sb_weights(z, carry, tri, None)
        acc = acc + _dot(w.astype(BF16), v_refs[p][0])
    o_ref[0] = jnp.sum(jnp.where(sel, acc, 0.0), axis=0, keepdims=True)


def sb_attn_sample(page_table, qb, bias_rows, tri, cache_kb, cache_vb):
    n, n_pages = page_table.shape
    w = SB_WIDTH

    def page_spec(p):
        return pl.BlockSpec((1, PAGE_SIZE, w), lambda i, pt, p=p: (pt[i * n_pages + p], 0, 0))

    grid_spec = pltpu.PrefetchScalarGridSpec(
        num_scalar_prefetch=1,
        grid=(n,),
        in_specs=[pl.BlockSpec((1, 1, w), lambda i, pt: (i, 0, 0)),
                  pl.BlockSpec((16, PAGE_SIZE), lambda i, pt: (0, 0)),
                  pl.BlockSpec((PAGE_SIZE, PAGE_SIZE), lambda i, pt: (0, 0))]
                 + [page_spec(p) for p in range(n_pages)] * 2,
        out_specs=pl.BlockSpec((1, 1, w), lambda i, pt: (i, 0, 0)),
    )
    out = pl.pallas_call(
        functools.partial(_sb_sample_kernel, n_pages=n_pages),
        grid_spec=grid_spec,
        out_shape=jax.ShapeDtypeStruct((n, 1, w), F32),
        compiler_params=_params(1),
        name="sb_attn_sample",
    )(page_table.reshape(-1), qb.reshape(n, 1, w), bias_rows, tri,
      *([cache_kb] * n_pages), *([cache_vb] * n_pages))
    return out.reshape(n, w)


def _strict_lower(n):
    j = lax.broadcasted_iota(jnp.int32, (n, n), 0)
    s = lax.broadcasted_iota(jnp.int32, (n, n), 1)
    return (j > s).astype(BF16)


def _row(v):
    return v.reshape(1, -1).astype(F32)


def _prep_weights(p):
    c = SB_WIDTH
    w_in = p['w_in_a'][0].astype(BF16)
    w_out_a = p['w_out_a'][0].astype(BF16)
    w_out_b = p['w_out_b'][0].astype(BF16)
    w_q = p['w_q_b'][0].astype(BF16)
    w_kv = p['w_kv'].astype(BF16)
    wr = jnp.pad(p['w_router'][0], ((0, 0), (0, LANES - N_EXPERTS)))
    wr_hi = wr.astype(BF16)
    wr_lo = (wr - wr_hi.astype(F32)).astype(BF16)
    conv_w = jnp.pad(p['conv_w'][0], ((0, HALO - CONV_WIDTH), (0, 0)))
    bias = p['sb_bias'][0].astype(F32)
    return dict(
        wa=w_in[:, :c], wb=w_in[:, c:2 * c], wqm_a=w_in[:, 2 * c:],
        wo_a1=w_out_a[:c], wo_a2=w_out_a[c:], wo_b1=w_out_b[:c], wo_b2=w_out_b[c:],
        wq=w_q[:, :c], wqm_b=w_q[:, c:], wk=w_kv[:, :c], wv=w_kv[:, c:],
        wmem=p['w_mem_kv'].astype(BF16),
        wg_d=p['w_gate_d'].astype(BF16), wu_d=p['w_up_d'].astype(BF16), wd_d=p['w_down_d'].astype(BF16),
        wg_e=p['w_gate_e'][0].astype(BF16), wu_e=p['w_up_e'][0].astype(BF16),
        wd_e=p['w_down_e'][0].astype(BF16),
        wr_hi=wr_hi, wr_lo=wr_lo,
        br=jnp.pad(p['b_router'][0], (0, LANES - N_EXPERTS)).reshape(1, LANES).astype(F32),
        conv_w=conv_w, conv_b=_row(p['conv_b'][0]), ln_g=_row(p['conv_ln_g'][0]),
        ln_b=_row(p['conv_ln_b'][0]),
        sb_bias=bias,
        sb_bias_rows=jnp.broadcast_to(jnp.pad(bias, (0, 16 - N_SB_HEADS))[:, None], (16, PAGE_SIZE)),
        g_kv=_row(p['g_kv']), g_mem=p['g_mem'],
        g_pre_mix=p['g_pre_mix'], g_post_mix=p['g_post_mix'],
        g_pre_ffn=p['g_pre_ffn'], g_post_ffn=p['g_post_ffn'],
    )


def _layer0_ffn_kv(x, o_main, o_mem, w, tm):
    x = outproj(o_main, o_mem, w['wo_a1'], w['wo_a2'], x, _row(w['g_post_mix'][0]), tm)
    x = ffn(x, _row(w['g_pre_ffn'][0]), _row(w['g_post_ffn'][0]), w['wr_hi'], w['wr_lo'], w['br'],
            w['wg_d'], w['wu_d'], w['wd_d'], tm, moe=False)
    k, v, kb, vb, qb, qm = kvq_proj(x, w['g_kv'], _row(w['g_pre_mix'][1]), w['wk'], w['wv'], w['wq'],
                                    w['wqm_b'], tm)
    return x, k, v, kb, vb, qb, qm


def _layer1_tail(x, o_main, o_mem, w, tm):
    x = outproj(o_main, o_mem, w['wo_b1'], w['wo_b2'], x, _row(w['g_post_mix'][1]), tm)
    return ffn(x, _row(w['g_pre_ffn'][1]), _row(w['g_post_ffn'][1]), w['wr_hi'], w['wr_lo'], w['br'],
               w['wg_e'], w['wu_e'], w['wd_e'], tm, moe=True)


def kernel(x_prompt, x_sample, mem_prompt, cache_k, cache_v, page_table, cache_mem_k, cache_mem_v, state_conv, w_in_a, conv_w, conv_b, conv_ln_g, conv_ln_b, w_out_a, w_q_b, w_out_b, sb_bias, g_kv, w_kv, g_mem, w_mem_kv, g_pre_mix, g_post_mix, g_pre_ffn, g_post_ffn, w_gate_d, w_up_d, w_down_d, w_router, b_router, w_gate_e, w_up_e, w_down_e):
    w = _prep_weights(dict(
        w_in_a=w_in_a, conv_w=conv_w, conv_b=conv_b, conv_ln_g=conv_ln_g, conv_ln_b=conv_ln_b,
        w_out_a=w_out_a, w_q_b=w_q_b, w_out_b=w_out_b, sb_bias=sb_bias, g_kv=g_kv, w_kv=w_kv,
        g_mem=g_mem, w_mem_kv=w_mem_kv, g_pre_mix=g_pre_mix, g_post_mix=g_post_mix,
        g_pre_ffn=g_pre_ffn, g_post_ffn=g_post_ffn, w_gate_d=w_gate_d, w_up_d=w_up_d,
        w_down_d=w_down_d, w_router=w_router, b_router=b_router, w_gate_e=w_gate_e, w_up_e=w_up_e,
        w_down_e=w_down_e))
    bp, t, d = x_prompt.shape
    bs = x_sample.shape[0]
    c = SB_WIDTH
    depth = g_mem.shape[0]

    tm = 512
    mem = mem_prompt.reshape(bp * N_MEM, d)
    mem_kv = [norm_matmul(mem, _row(w['g_mem'][l]), w['wmem'][l], 256) for l in range(depth)]
    mem_k = [kv[:, :MEM_WIDTH] for kv in mem_kv]
    mem_v = [kv[:, MEM_WIDTH:] for kv in mem_kv]
    mem_kb = [a.astype(BF16).reshape(bp, N_MEM, MEM_WIDTH) for a in mem_k]
    mem_vb = [a.astype(BF16).reshape(bp, N_MEM, MEM_WIDTH) for a in mem_v]

    x = x_prompt.reshape(bp * t, d)
    u_p, qm = inproj_glu(x, _row(w['g_pre_mix'][0]), w['wa'], w['wb'], w['wqm_a'], tm)
    o_main = conv_prompt(u_p.reshape(bp, t, c), w['conv_w'], w['conv_b'], w['ln_g'], w['ln_b'], 256)
    o_mem = mem_attn_prompt(qm.reshape(bp, t, MEM_WIDTH), mem_kb[0], mem_vb[0], tm)
    x, k_p, v_p, kb, vb, qb, qm = _layer0_ffn_kv(
        x, o_main.reshape(bp * t, c), o_mem.reshape(bp * t, MEM_WIDTH), w, tm)
    o_main = sb_attn_prompt(w['sb_bias'], qb.reshape(bp, t, c), kb.reshape(bp, t, c),
                            vb.reshape(bp, t, c), _strict_lower(SB_BLOCK))
    o_mem = mem_attn_prompt(qm.reshape(bp, t, MEM_WIDTH), mem_kb[1], mem_vb[1], tm)
    y_p = _layer1_tail(x, o_main.reshape(bp * t, c), o_mem.reshape(bp * t, MEM_WIDTH), w, tm)

    ts = bs
    n_phys = cache_k.shape[0]
    cache_kb = cache_k.reshape(n_phys, PAGE_SIZE, c).astype(BF16)
    cache_vb = cache_v.reshape(n_phys, PAGE_SIZE, c).astype(BF16)
    cmk = cache_mem_k.reshape(depth, bs, N_MEM, MEM_WIDTH).astype(BF16)
    cmv = cache_mem_v.reshape(depth, bs, N_MEM, MEM_WIDTH).astype(BF16)

    xs = x_sample.reshape(bs, d)
    u_s, qm = inproj_glu(xs, _row(w['g_pre_mix'][0]), w['wa'], w['wb'], w['wqm_a'], ts)
    o_main, conv_s = conv_sample(state_conv[0], u_s, w['conv_w'], w['conv_b'], w['ln_g'], w['ln_b'], 8)
    o_mem = mem_attn_sample(qm, cmk[0], cmv[0], 8)
    xs, k_s, v_s, _, _, qb, qm = _layer0_ffn_kv(xs, o_main, o_mem, w, ts)
    o_main = sb_attn_sample(page_table, qb, w['sb_bias_rows'], _strict_lower(PAGE_SIZE), cache_kb, cache_vb)
    o_mem = mem_attn_sample(qm, cmk[1], cmv[1], 8)
    y_s = _layer1_tail(xs, o_main, o_mem, w, ts)

    n_prev = CONV_WIDTH - 1
    heads = (N_SB_HEADS, HEAD_DIM)
    mheads = (N_MEM_HEADS, HEAD_DIM)
    return (
        y_p.reshape(bp, t, d),
        y_s.reshape(bs, 1, d),
        u_p.reshape(bp, t, c)[:, t - n_prev:][None],
        k_p.reshape(bp, t, *heads),
        v_p.reshape(bp, t, *heads),
        jnp.stack([a.reshape(bp, N_MEM, *mheads) for a in mem_k]),
        jnp.stack([a.reshape(bp, N_MEM, *mheads) for a in mem_v]),
        conv_s[None],
        k_s.reshape(bs, 1, *heads),
        v_s.reshape(bs, 1, *heads),
    )
```

```python
import functools

import jax
import jax.numpy as jnp
from jax import lax
from jax.experimental import pallas as pl
from jax.experimental.pallas import tpu as pltpu

F32 = jnp.float32
BF16 = jnp.bfloat16

D_MODEL = 1024
HEAD_DIM = 64
N_SB_HEADS = 12
SB_WIDTH = N_SB_HEADS * HEAD_DIM
N_MEM_HEADS = 4
MEM_WIDTH = N_MEM_HEADS * HEAD_DIM
N_MEM = 256
CONV_WIDTH = 31
N_EXPERTS = 8
PAGE_SIZE = 128
EPS = 1e-6
QK_SCALE = HEAD_DIM ** -0.5

LANES = 128
HALO = 32
CONV_CHUNK = 32
SB_BLOCK = 256
SB_ROW_SPLIT = 1
VMEM_LIMIT = 56 * 1024 * 1024

NT_DIMS = (((1,), (1,)), ((), ()))


def _dot(a, b):
    return jnp.dot(a, b, preferred_element_type=F32)


def _dot_nt(a, b):
    return lax.dot_general(a, b, NT_DIMS, preferred_element_type=F32)


def _inv_rms(x):
    return lax.rsqrt(jnp.mean(x * x, axis=-1, keepdims=True) + EPS)


def _params(n_parallel, n_arbitrary=0):
    sem = ("parallel",) * n_parallel + ("arbitrary",) * n_arbitrary
    return pltpu.CompilerParams(dimension_semantics=sem, vmem_limit_bytes=VMEM_LIMIT)


def _row_spec(tm, width):
    return pl.BlockSpec((tm, width), lambda i: (i, 0))


def _full_spec(shape):
    return pl.BlockSpec(shape, lambda i: (0,) * len(shape))


def _norm_matmul_kernel(x_ref, g_ref, w_ref, o_ref):
    x = x_ref[...]
    h = (x * _inv_rms(x) * g_ref[...]).astype(BF16)
    o_ref[...] = _dot(h, w_ref[...])


def norm_matmul(x, g, w, tm):
    n, d = x.shape
    m = w.shape[1]
    return pl.pallas_call(
        _norm_matmul_kernel,
        grid=(n // tm,),
        in_specs=[_row_spec(tm, d), _full_spec((1, d)), _full_spec((d, m))],
        out_specs=_row_spec(tm, m),
        out_shape=jax.ShapeDtypeStruct((n, m), F32),
        compiler_params=_params(1),
        name="norm_matmul",
    )(x, g, w)


def _inproj_kernel(x_ref, g_ref, wa_ref, wb_ref, wq_ref, u_ref, qm_ref):
    x = x_ref[...]
    h = (x * _inv_rms(x) * g_ref[...]).astype(BF16)
    a = _dot(h, wa_ref[...])
    b = _dot(h, wb_ref[...])
    u_ref[...] = a * jax.nn.sigmoid(b)
    qm_ref[...] = _dot(h, wq_ref[...])


def inproj_glu(x, g, wa, wb, wq, tm):
    n, d = x.shape
    c = wa.shape[1]
    return pl.pallas_call(
        _inproj_kernel,
        grid=(n // tm,),
        in_specs=[_row_spec(tm, d), _full_spec((1, d)), _full_spec((d, c)), _full_spec((d, c)),
                  _full_spec((d, MEM_WIDTH))],
        out_specs=[_row_spec(tm, c), _row_spec(tm, MEM_WIDTH)],
        out_shape=[jax.ShapeDtypeStruct((n, c), F32), jax.ShapeDtypeStruct((n, MEM_WIDTH), F32)],
        compiler_params=_params(1),
        name="inproj_glu",
    )(x, g, wa, wb, wq)


def _kvq_kernel(x_ref, gkv_ref, gq_ref, wk_ref, wv_ref, wq_ref, wqm_ref,
                k_ref, v_ref, kb_ref, vb_ref, qb_ref, qm_ref):
    x = x_ref[...]
    xn = x * _inv_rms(x)
    hkv = (xn * gkv_ref[...]).astype(BF16)
    hq = (xn * gq_ref[...]).astype(BF16)
    k = _dot(hkv, wk_ref[...])
    v = _dot(hkv, wv_ref[...])
    k_ref[...] = k
    v_ref[...] = v
    kb_ref[...] = k.astype(BF16)
    vb_ref[...] = v.astype(BF16)
    qb_ref[...] = (_dot(hq, wq_ref[...]) * QK_SCALE).astype(BF16)
    qm_ref[...] = _dot(hq, wqm_ref[...])


def kvq_proj(x, gkv, gq, wk, wv, wq, wqm, tm):
    n, d = x.shape
    w = SB_WIDTH
    return pl.pallas_call(
        _kvq_kernel,
        grid=(n // tm,),
        in_specs=[_row_spec(tm, d), _full_spec((1, d)), _full_spec((1, d)), _full_spec((d, w)),
                  _full_spec((d, w)), _full_spec((d, w)), _full_spec((d, MEM_WIDTH))],
        out_specs=[_row_spec(tm, w)] * 5 + [_row_spec(tm, MEM_WIDTH)],
        out_shape=[jax.ShapeDtypeStruct((n, w), F32), jax.ShapeDtypeStruct((n, w), F32),
                   jax.ShapeDtypeStruct((n, w), BF16), jax.ShapeDtypeStruct((n, w), BF16),
                   jax.ShapeDtypeStruct((n, w), BF16), jax.ShapeDtypeStruct((n, MEM_WIDTH), F32)],
        compiler_params=_params(1),
        name="kvq_proj",
    )(x, gkv, gq, wk, wv, wq, wqm)


def _ln_swish(c, lg, lb):
    mu = jnp.mean(c, axis=-1, keepdims=True)
    xc = c - mu
    y = xc * lax.rsqrt(jnp.mean(xc * xc, axis=-1, keepdims=True) + EPS) * lg + lb
    return y * jax.nn.sigmoid(y)


def _conv_prompt_kernel(u_ref, halo_ref, w_ref, cb_ref, lg_ref, lb_ref, o_ref, ext_ref, *, tt):
    i = pl.program_id(1)
    ext_ref[0:HALO, :] = jnp.where(i == 0, 0.0, halo_ref[0])
    ext_ref[HALO:HALO + tt, :] = u_ref[0]
    first = HALO - (CONV_WIDTH - 1)
    for r in range(tt // CONV_CHUNK):
        base = r * CONV_CHUNK + first
        acc = ext_ref[base:base + CONV_CHUNK, :] * w_ref[0:1, :]
        for k in range(1, CONV_WIDTH):
            acc = acc + ext_ref[base + k:base + k + CONV_CHUNK, :] * w_ref[k:k + 1, :]
        c = acc + cb_ref[...]
        o_ref[0, r * CONV_CHUNK:(r + 1) * CONV_CHUNK, :] = _ln_swish(c, lg_ref[...], lb_ref[...])


def conv_prompt(u, w, cb, lg, lb, tt):
    b, t, c = u.shape
    per = tt // HALO
    return pl.pallas_call(
        functools.partial(_conv_prompt_kernel, tt=tt),
        grid=(b, t // tt),
        in_specs=[pl.BlockSpec((1, tt, c), lambda bi, i: (bi, i, 0)),
                  pl.BlockSpec((1, HALO, c), lambda bi, i: (bi, jnp.maximum(i * per - 1, 0), 0)),
                  pl.BlockSpec((HALO, c), lambda bi, i: (0, 0)),
                  pl.BlockSpec((1, c), lambda bi, i: (0, 0)),
                  pl.BlockSpec((1, c), lambda bi, i: (0, 0)),
                  pl.BlockSpec((1, c), lambda bi, i: (0, 0))],
        out_specs=pl.BlockSpec((1, tt, c), lambda bi, i: (bi, i, 0)),
        out_shape=jax.ShapeDtypeStruct((b, t, c), F32),
        scratch_shapes=[pltpu.VMEM((HALO + tt, c), F32)],
        compiler_params=_params(2),
        name="conv_prompt",
    )(u, u, w, cb, lg, lb)


def _conv_sample_kernel(s_ref, u_ref, w_ref, cb_ref, lg_ref, lb_ref, o_ref, ns_ref, *, bs):
    n_prev = CONV_WIDTH - 1
    for b in range(bs):
        s = s_ref[b]
        u = u_ref[b:b + 1, :]
        c = jnp.sum(s * w_ref[0:n_prev, :], axis=0, keepdims=True) + u * w_ref[n_prev:n_prev + 1, :]
        c = c + cb_ref[...]
        o_ref[b:b + 1, :] = _ln_swish(c, lg_ref[...], lb_ref[...])
        ns_ref[b, 0:n_prev - 1, :] = s_ref[b, 1:n_prev, :]
        ns_ref[b, n_prev - 1:n_prev, :] = u


def conv_sample(state, u, w, cb, lg, lb, bs):
    n, n_prev, c = state.shape
    return pl.pallas_call(
        functools.partial(_conv_sample_kernel, bs=bs),
        grid=(n // bs,),
        in_specs=[pl.BlockSpec((bs, n_prev, c), lambda i: (i, 0, 0)),
                  _row_spec(bs, c), _full_spec((HALO, c)), _full_spec((1, c)), _full_spec((1, c)),
                  _full_spec((1, c))],
        out_specs=[_row_spec(bs, c), pl.BlockSpec((bs, n_prev, c), lambda i: (i, 0, 0))],
        out_shape=[jax.ShapeDtypeStruct((n, c), F32), jax.ShapeDtypeStruct((n, n_prev, c), F32)],
        compiler_params=_params(1),
        name="conv_sample",
    )(state, u, w, cb, lg, lb)


def _softmax_rows(s):
    e = jnp.exp(s - jnp.max(s, axis=-1, keepdims=True))
    return e / jnp.sum(e, axis=-1, keepdims=True)


def _mem_attn_prompt_kernel(q_ref, mk_ref, mv_ref, o_ref):
    q = q_ref[0] * QK_SCALE
    mk = mk_ref[0]
    mv = mv_ref[0]
    head = lax.broadcasted_iota(jnp.int32, q.shape, 1) // HEAD_DIM
    acc = jnp.zeros(q.shape, F32)
    for h in range(N_MEM_HEADS):
        sel = head == h
        s = _dot_nt(jnp.where(sel, q, 0.0).astype(BF16), mk)
        o = _dot(_softmax_rows(s).astype(BF16), mv)
        acc = jnp.where(sel, o, acc)
    o_ref[0] = acc


def mem_attn_prompt(qm, mk, mv, tt):
    b, t, w = qm.shape
    return pl.pallas_call(
        _mem_attn_prompt_kernel,
        grid=(b, t // tt),
        in_specs=[pl.BlockSpec((1, tt, w), lambda bi, i: (bi, i, 0)),
                  pl.BlockSpec((1, N_MEM, w), lambda bi, i: (bi, 0, 0)),
                  pl.BlockSpec((1, N_MEM, w), lambda bi, i: (bi, 0, 0))],
        out_specs=pl.BlockSpec((1, tt, w), lambda bi, i: (bi, i, 0)),
        out_shape=jax.ShapeDtypeStruct((b, t, w), F32),
        compiler_params=_params(2),
        name="mem_attn_prompt",
    )(qm, mk, mv)


def _mem_attn_sample_kernel(q_ref, mk_ref, mv_ref, o_ref, *, bs):
    rows = 8
    row = lax.broadcasted_iota(jnp.int32, (rows, MEM_WIDTH), 0)
    head = lax.broadcasted_iota(jnp.int32, (rows, MEM_WIDTH), 1) // HEAD_DIM
    sel = row == head
    for b in range(bs):
        q = jnp.broadcast_to(q_ref[b:b + 1, :] * QK_SCALE, (rows, MEM_WIDTH))
        qh = jnp.where(sel, q, 0.0).astype(BF16)
        s = _dot(qh, mk_ref[b].astype(BF16))
        o = _dot_nt(_softmax_rows(s).astype(BF16), mv_ref[b].astype(BF16))
        o_ref[b:b + 1, :] = jnp.sum(jnp.where(sel, o, 0.0), axis=0, keepdims=True)


def mem_attn_sample(qm, mk, mv, bs):
    n, w = qm.shape
    return pl.pallas_call(
        functools.partial(_mem_attn_sample_kernel, bs=bs),
        grid=(n // bs,),
        in_specs=[_row_spec(bs, w),
                  pl.BlockSpec((bs, w, N_MEM), lambda i: (i, 0, 0)),
                  pl.BlockSpec((bs, w, N_MEM), lambda i: (i, 0, 0))],
        out_specs=_row_spec(bs, w),
        out_shape=jax.ShapeDtypeStruct((n, w), F32),
        compiler_params=_params(1),
        name="mem_attn_sample",
    )(qm, mk, mv)


def _outproj_kernel(om_ref, omem_ref, w1_ref, w2_ref, x_ref, g_ref, o_ref):
    mix = _dot(om_ref[...].astype(BF16), w1_ref[...]) + _dot(omem_ref[...].astype(BF16), w2_ref[...])
    o_ref[...] = x_ref[...] + mix * _inv_rms(mix) * g_ref[...]


def outproj(o_main, o_mem, w1, w2, x, g, tm):
    n, d = x.shape
    return pl.pallas_call(
        _outproj_kernel,
        grid=(n // tm,),
        in_specs=[_row_spec(tm, SB_WIDTH), _row_spec(tm, MEM_WIDTH), _full_spec((SB_WIDTH, d)),
                  _full_spec((MEM_WIDTH, d)), _row_spec(tm, d), _full_spec((1, d))],
        out_specs=_row_spec(tm, d),
        out_shape=jax.ShapeDtypeStruct((n, d), F32),
        compiler_params=_params(1),
        name="outproj",
    )(o_main, o_mem, w1, w2, x, g)


def _top2_gates(logits):
    lane = lax.broadcasted_iota(jnp.int32, logits.shape, 1).astype(F32)
    valid = lane < N_EXPERTS
    lg = jnp.where(valid, logits, -1e30)
    e = jnp.where(valid, jnp.exp(lg - jnp.max(lg, axis=-1, keepdims=True)), 0.0)
    p = e / jnp.sum(e, axis=-1, keepdims=True)
    p = jnp.where(valid, p, -1.0)
    m1 = jnp.max(p, axis=-1, keepdims=True)
    i1 = jnp.min(jnp.where(p == m1, lane, float(LANES)), axis=-1, keepdims=True)
    p2 = jnp.where(lane == i1, -1.0, p)
    m2 = jnp.max(p2, axis=-1, keepdims=True)
    i2 = jnp.min(jnp.where(p2 == m2, lane, float(LANES)), axis=-1, keepdims=True)
    tot = m1 + m2
    return jnp.where(lane == i1, m1 / tot, jnp.where(lane == i2, m2 / tot, 0.0))


def _ffn_kernel(x_ref, gpre_ref, gpost_ref, wrh_ref, wrl_ref, br_ref, wg_ref, wu_ref, wd_ref,
                o_ref, h_ref, acc_ref, gate_ref, *, moe, n_steps):
    j = pl.program_id(1)

    @pl.when(j == 0)
    def _():
        x = x_ref[...]
        h = x * _inv_rms(x) * gpre_ref[...]
        h_hi = h.astype(BF16)
        h_ref[...] = h_hi
        acc_ref[...] = jnp.zeros_like(acc_ref)
        if moe:
            h_lo = (h - h_hi.astype(F32)).astype(BF16)
            logits = (_dot(h_hi, wrh_ref[...]) + _dot(h_hi, wrl_ref[...]) + _dot(h_lo, wrh_ref[...])
                      + br_ref[...])
            gate_ref[...] = _top2_gates(logits)

    h = h_ref[...]
    g = _dot(h, wg_ref[0])
    u = _dot(h, wu_ref[0])
    a = ((g * jax.nn.sigmoid(g)) * u).astype(BF16)
    y = _dot(a, wd_ref[0])
    if moe:
        lane = lax.broadcasted_iota(jnp.int32, gate_ref.shape, 1)
        y = y * jnp.sum(jnp.where(lane == j, gate_ref[...], 0.0), axis=-1, keepdims=True)
    acc_ref[...] += y

    @pl.when(j == n_steps - 1)
    def _():
        f = acc_ref[...]
        o_ref[...] = x_ref[...] + f * _inv_rms(f) * gpost_ref[...]


def ffn(x, gpre, gpost, wrh, wrl, br, wg, wu, wd, tm, moe):
    n, d = x.shape
    if moe:
        n_steps, f = wg.shape[0], wg.shape[2]
        up_map = lambda i, j: (j, 0, 0)
        down_map = lambda i, j: (j, 0, 0)
    else:
        f = wd.shape[1] // 2
        n_steps = 2
        up_map = lambda i, j: (0, 0, j)
        down_map = lambda i, j: (0, j, 0)
    const = lambda i, j: (0, 0)
    return pl.pallas_call(
        functools.partial(_ffn_kernel, moe=moe, n_steps=n_steps),
        grid=(n // tm, n_steps),
        in_specs=[pl.BlockSpec((tm, d), lambda i, j: (i, 0)),
                  pl.BlockSpec((1, d), const), pl.BlockSpec((1, d), const),
                  pl.BlockSpec((d, LANES), const), pl.BlockSpec((d, LANES), const),
                  pl.BlockSpec((1, LANES), const),
                  pl.BlockSpec((1, d, f), up_map), pl.BlockSpec((1, d, f), up_map),
                  pl.BlockSpec((1, f, d), down_map)],
        out_specs=pl.BlockSpec((tm, d), lambda i, j: (i, 0)),
        out_shape=jax.ShapeDtypeStruct((n, d), F32),
        scratch_shapes=[pltpu.VMEM((tm, d), BF16), pltpu.VMEM((tm, d), F32),
                        pltpu.VMEM((tm, LANES), F32)],
        compiler_params=_params(1, 1),
        name="ffn_moe" if moe else "ffn_dense",
    )(x, gpre, gpost, wrh, wrl, br, wg, wu, wd)


def _sb_weights(z, carry, tri, causal):
    soft = jnp.log(1.0 + jnp.exp(-jnp.abs(z)))
    log_beta = jnp.minimum(z, 0.0) - soft
    log_keep = log_beta - z
    if causal is not None:
        log_keep = jnp.where(causal, log_keep, 0.0)
    between = _dot(log_keep.astype(BF16), tri) + carry
    w = jnp.exp(log_beta + between)
    if causal is not None:
        w = jnp.where(causal, w, 0.0)
    return w, carry + jnp.sum(log_keep, axis=-1, keepdims=True)


def _sb_prompt_kernel(bias_ref, q_ref, k_ref, v_ref, tri_ref, o_ref):
    hp = pl.program_id(1)
    qi = pl.program_id(2)
    blk = SB_BLOCK
    q = q_ref[0].astype(F32)
    tri = tri_ref[...]
    head = lax.broadcasted_iota(jnp.int32, (blk, LANES), 1) // HEAD_DIM
    row = lax.broadcasted_iota(jnp.int32, (blk, blk), 0)
    col = lax.broadcasted_iota(jnp.int32, (blk, blk), 1)
    causal = col < row
    n_heads = LANES // HEAD_DIM
    qhs = [jnp.where(head == hh, q, 0.0).astype(BF16) for hh in range(n_heads)]
    biases = [bias_ref[hp * n_heads + hh] for hh in range(n_heads)]

    def block(kb, state, mask):
        start = pl.multiple_of(kb * blk, blk)
        kblk = k_ref[0, pl.ds(start, blk), :]
        vblk = v_ref[0, pl.ds(start, blk), :]
        new = []
        for hh in range(n_heads):
            for r in range(SB_ROW_SPLIT):
                rows = slice(r * sub, (r + 1) * sub)
                carry, acc = state[hh * SB_ROW_SPLIT + r]
                z = _dot_nt(qhs[hh][rows], kblk) + biases[hh]
                w, carry = _sb_weights(z, carry, tri, None if mask is None else mask[rows])
                new.append((carry, acc + _dot(w.astype(BF16), vblk)))
        return tuple(new)

    sub = blk // SB_ROW_SPLIT
    init = tuple((jnp.zeros((sub, 1), F32), jnp.zeros((sub, LANES), F32))
                 for _ in range(n_heads * SB_ROW_SPLIT))
    state = block(qi, init, causal)
    state = lax.fori_loop(0, qi, lambda i, st: block(qi - 1 - i, st, None), state)
    accs = [jnp.concatenate([state[hh * SB_ROW_SPLIT + r][1] for r in range(SB_ROW_SPLIT)], axis=0)
            for hh in range(n_heads)]
    o_ref[0] = jnp.where(head == 0, accs[0], accs[1])


def sb_attn_prompt(bias, qb, kb, vb, tri):
    b, t, w = qb.shape
    blk = SB_BLOCK
    grid_spec = pltpu.PrefetchScalarGridSpec(
        num_scalar_prefetch=0,
        grid=(b, w // LANES, t // blk),
        in_specs=[pl.BlockSpec(memory_space=pltpu.SMEM),
                  pl.BlockSpec((1, blk, LANES), lambda bi, hp, qi: (bi, qi, hp)),
                  pl.BlockSpec((1, t, LANES), lambda bi, hp, qi: (bi, 0, hp)),
                  pl.BlockSpec((1, t, LANES), lambda bi, hp, qi: (bi, 0, hp)),
                  pl.BlockSpec((blk, blk), lambda bi, hp, qi: (0, 0))],
        out_specs=pl.BlockSpec((1, blk, LANES), lambda bi, hp, qi: (bi, qi, hp)),
    )
    return pl.pallas_call(
        _sb_prompt_kernel,
        grid_spec=grid_spec,
        out_shape=jax.ShapeDtypeStruct((b, t, w), F32),
        compiler_params=_params(3),
        name="sb_attn_prompt",
    )(bias, qb, kb, vb, tri)


def _sb_sample_kernel(pt_ref, q_ref, bias_ref, tri_ref, *refs, n_pages):
    k_refs = refs[:n_pages]
    v_refs = refs[n_pages:2 * n_pages]
    o_ref = refs[2 * n_pages]
    rows = 16
    row = lax.broadcasted_iota(jnp.int32, (rows, SB_WIDTH), 0)
    head = lax.broadcasted_iota(jnp.int32, (rows, SB_WIDTH), 1) // HEAD_DIM
    sel = row == head
    q = jnp.broadcast_to(q_ref[0].astype(F32), (rows, SB_WIDTH))
    qh = jnp.where(sel, q, 0.0).astype(BF16)
    tri = tri_ref[...]
    bias = bias_ref[...]
    carry = jnp.zeros((rows, 1), F32)
    acc = jnp.zeros((rows, SB_WIDTH), F32)
    for p in reversed(range(n_pages)):
        z = _dot(qh, k_refs[p][0].astype(BF16)) + bias
        w, carry = _sb_weights(z, carry, tri, None)
        acc = acc + _dot_nt(w.astype(BF16), v_refs[p][0].astype(BF16))
    o_ref[0] = jnp.sum(jnp.where(sel, acc, 0.0), axis=0, keepdims=True)


def sb_attn_sample(page_table, qb, bias_rows, tri, cache_kt, cache_vt):
    n, n_pages = page_table.shape
    w = SB_WIDTH

    def page_spec(p):
        return pl.BlockSpec((1, w, PAGE_SIZE), lambda i, pt, p=p: (pt[i * n_pages + p], 0, 0))

    grid_spec = pltpu.PrefetchScalarGridSpec(
        num_scalar_prefetch=1,
        grid=(n,),
        in_specs=[pl.BlockSpec((1, 1, w), lambda i, pt: (i, 0, 0)),
                  pl.BlockSpec((16, PAGE_SIZE), lambda i, pt: (0, 0)),
                  pl.BlockSpec((PAGE_SIZE, PAGE_SIZE), lambda i, pt: (0, 0))]
                 + [page_spec(p) for p in range(n_pages)] * 2,
        out_specs=pl.BlockSpec((1, 1, w), lambda i, pt: (i, 0, 0)),
    )
    out = pl.pallas_call(
        functools.partial(_sb_sample_kernel, n_pages=n_pages),
        grid_spec=grid_spec,
        out_shape=jax.ShapeDtypeStruct((n, 1, w), F32),
        compiler_params=_params(1),
        name="sb_attn_sample",
    )(page_table.reshape(-1), qb.reshape(n, 1, w), bias_rows, tri,
      *([cache_kt] * n_pages), *([cache_vt] * n_pages))
    return out.reshape(n, w)


def _strict_lower(n):
    j = lax.broadcasted_iota(jnp.int32, (n, n), 0)
    s = lax.broadcasted_iota(jnp.int32, (n, n), 1)
    return (j > s).astype(BF16)


def _row(v):
    return v.reshape(1, -1).astype(F32)


def _prep_weights(p):
    c = SB_WIDTH
    w_in = p['w_in_a'][0].astype(BF16)
    w_out_a = p['w_out_a'][0].astype(BF16)
    w_out_b = p['w_out_b'][0].astype(BF16)
    w_q = p['w_q_b'][0].astype(BF16)
    w_kv = p['w_kv'].astype(BF16)
    wr = jnp.pad(p['w_router'][0], ((0, 0), (0, LANES - N_EXPERTS)))
    wr_hi = wr.astype(BF16)
    wr_lo = (wr - wr_hi.astype(F32)).astype(BF16)
    conv_w = jnp.pad(p['conv_w'][0], ((0, HALO - CONV_WIDTH), (0, 0)))
    bias = p['sb_bias'][0].astype(F32)
    return dict(
        wa=w_in[:, :c], wb=w_in[:, c:2 * c], wqm_a=w_in[:, 2 * c:],
        wo_a1=w_out_a[:c], wo_a2=w_out_a[c:], wo_b1=w_out_b[:c], wo_b2=w_out_b[c:],
        wq=w_q[:, :c], wqm_b=w_q[:, c:], wk=w_kv[:, :c], wv=w_kv[:, c:],
        wmem=p['w_mem_kv'].astype(BF16),
        wg_d=p['w_gate_d'].astype(BF16), wu_d=p['w_up_d'].astype(BF16), wd_d=p['w_down_d'].astype(BF16),
        wg_e=p['w_gate_e'][0].astype(BF16), wu_e=p['w_up_e'][0].astype(BF16),
        wd_e=p['w_down_e'][0].astype(BF16),
        wr_hi=wr_hi, wr_lo=wr_lo,
        br=jnp.pad(p['b_router'][0], (0, LANES - N_EXPERTS)).reshape(1, LANES).astype(F32),
        conv_w=conv_w, conv_b=_row(p['conv_b'][0]), ln_g=_row(p['conv_ln_g'][0]),
        ln_b=_row(p['conv_ln_b'][0]),
        sb_bias=bias,
        sb_bias_rows=jnp.broadcast_to(jnp.pad(bias, (0, 16 - N_SB_HEADS))[:, None], (16, PAGE_SIZE)),
        g_kv=_row(p['g_kv']), g_mem=p['g_mem'],
        g_pre_mix=p['g_pre_mix'], g_post_mix=p['g_post_mix'],
        g_pre_ffn=p['g_pre_ffn'], g_post_ffn=p['g_post_ffn'],
    )


def _layer0_ffn_kv(x, o_main, o_mem, w, tm):
    x = outproj(o_main, o_mem, w['wo_a1'], w['wo_a2'], x, _row(w['g_post_mix'][0]), tm)
    x = ffn(x, _row(w['g_pre_ffn'][0]), _row(w['g_post_ffn'][0]), w['wr_hi'], w['wr_lo'], w['br'],
            w['wg_d'], w['wu_d'], w['wd_d'], tm, moe=False)
    k, v, kb, vb, qb, qm = kvq_proj(x, w['g_kv'], _row(w['g_pre_mix'][1]), w['wk'], w['wv'], w['wq'],
                                    w['wqm_b'], tm)
    return x, k, v, kb, vb, qb, qm


def _layer1_tail(x, o_main, o_mem, w, tm):
    x = outproj(o_main, o_mem, w['wo_b1'], w['wo_b2'], x, _row(w['g_post_mix'][1]), tm)
    return ffn(x, _row(w['g_pre_ffn'][1]), _row(w['g_post_ffn'][1]), w['wr_hi'], w['wr_lo'], w['br'],
               w['wg_e'], w['wu_e'], w['wd_e'], tm, moe=True)


def kernel(x_prompt, x_sample, mem_prompt, cache_k, cache_v, page_table, cache_mem_k, cache_mem_v, state_conv, w_in_a, conv_w, conv_b, conv_ln_g, conv_ln_b, w_out_a, w_q_b, w_out_b, sb_bias, g_kv, w_kv, g_mem, w_mem_kv, g_pre_mix, g_post_mix, g_pre_ffn, g_post_ffn, w_gate_d, w_up_d, w_down_d, w_router, b_router, w_gate_e, w_up_e, w_down_e):
    w = _prep_weights(dict(
        w_in_a=w_in_a, conv_w=conv_w, conv_b=conv_b, conv_ln_g=conv_ln_g, conv_ln_b=conv_ln_b,
        w_out_a=w_out_a, w_q_b=w_q_b, w_out_b=w_out_b, sb_bias=sb_bias, g_kv=g_kv, w_kv=w_kv,
        g_mem=g_mem, w_mem_kv=w_mem_kv, g_pre_mix=g_pre_mix, g_post_mix=g_post_mix,
        g_pre_ffn=g_pre_ffn, g_post_ffn=g_post_ffn, w_gate_d=w_gate_d, w_up_d=w_up_d,
        w_down_d=w_down_d, w_router=w_router, b_router=b_router, w_gate_e=w_gate_e, w_up_e=w_up_e,
        w_down_e=w_down_e))
    bp, t, d = x_prompt.shape
    bs = x_sample.shape[0]
    c = SB_WIDTH
    depth = g_mem.shape[0]

    tm = 512
    mem = mem_prompt.reshape(bp * N_MEM, d)
    mem_kv = [norm_matmul(mem, _row(w['g_mem'][l]), w['wmem'][l], 256) for l in range(depth)]
    mem_k = [kv[:, :MEM_WIDTH] for kv in mem_kv]
    mem_v = [kv[:, MEM_WIDTH:] for kv in mem_kv]
    mem_kb = [a.astype(BF16).reshape(bp, N_MEM, MEM_WIDTH) for a in mem_k]
    mem_vb = [a.astype(BF16).reshape(bp, N_MEM, MEM_WIDTH) for a in mem_v]

    x = x_prompt.reshape(bp * t, d)
    u_p, qm = inproj_glu(x, _row(w['g_pre_mix'][0]), w['wa'], w['wb'], w['wqm_a'], tm)
    o_main = conv_prompt(u_p.reshape(bp, t, c), w['conv_w'], w['conv_b'], w['ln_g'], w['ln_b'], 256)
    o_mem = mem_attn_prompt(qm.reshape(bp, t, MEM_WIDTH), mem_kb[0], mem_vb[0], tm)
    x, k_p, v_p, kb, vb, qb, qm = _layer0_ffn_kv(
        x, o_main.reshape(bp * t, c), o_mem.reshape(bp * t, MEM_WIDTH), w, tm)
    o_main = sb_attn_prompt(w['sb_bias'], qb.reshape(bp, t, c), kb.reshape(bp, t, c),
                            vb.reshape(bp, t, c), _strict_lower(SB_BLOCK))
    o_mem = mem_attn_prompt(qm.reshape(bp, t, MEM_WIDTH), mem_kb[1], mem_vb[1], tm)
    y_p = _layer1_tail(x, o_main.reshape(bp * t, c), o_mem.reshape(bp * t, MEM_WIDTH), w, tm)

    ts = bs
    n_phys = cache_k.shape[0]
    cache_kt = jnp.transpose(cache_k, (0, 2, 3, 1)).reshape(n_phys, c, PAGE_SIZE)
    cache_vt = jnp.transpose(cache_v, (0, 2, 3, 1)).reshape(n_phys, c, PAGE_SIZE)
    cmk = jnp.transpose(cache_mem_k, (0, 1, 3, 4, 2)).reshape(depth, bs, MEM_WIDTH, N_MEM)
    cmv = jnp.transpose(cache_mem_v, (0, 1, 3, 4, 2)).reshape(depth, bs, MEM_WIDTH, N_MEM)

    xs = x_sample.reshape(bs, d)
    u_s, qm = inproj_glu(xs, _row(w['g_pre_mix'][0]), w['wa'], w['wb'], w['wqm_a'], ts)
    o_main, conv_s = conv_sample(state_conv[0], u_s, w['conv_w'], w['conv_b'], w['ln_g'], w['ln_b'], 8)
    o_mem = mem_attn_sample(qm, cmk[0], cmv[0], 8)
    xs, k_s, v_s, _, _, qb, qm = _layer0_ffn_kv(xs, o_main, o_mem, w, ts)
    o_main = sb_attn_sample(page_table, qb, w['sb_bias_rows'], _strict_lower(PAGE_SIZE), cache_kt, cache_vt)
    o_mem = mem_attn_sample(qm, cmk[1], cmv[1], 8)
    y_s = _layer1_tail(xs, o_main, o_mem, w, ts)

    n_prev = CONV_WIDTH - 1
    heads = (N_SB_HEADS, HEAD_DIM)
    mheads = (N_MEM_HEADS, HEAD_DIM)
    return (
        y_p.reshape(bp, t, d),
        y_s.reshape(bs, 1, d),
        u_p.reshape(bp, t, c)[:, t - n_prev:][None],
        k_p.reshape(bp, t, *heads),
        v_p.reshape(bp, t, *heads),
        jnp.stack([a.reshape(bp, N_MEM, *mheads) for a in mem_k]),
        jnp.stack([a.reshape(bp, N_MEM, *mheads) for a in mem_v]),
        conv_s[None],
        k_s.reshape(bs, 1, *heads),
        v_s.reshape(bs, 1, *heads),
    )
```
